```python
import math
import jax, jax.numpy as jnp
from jax import lax
import numpy as np

D_MODEL = 2048
BATCH = 1
SEQ = 16384
DEPTH = 2
DEC_BATCH = 16
DEC_SEQ = 2048
PAST_LEN = 128

ATT_HEADS = 8
ATT_HEAD_DIM = 128
ATT_WIDTH = ATT_HEADS * ATT_HEAD_DIM
DILATION_PAIRS = ((128, 1), (512, 4), (2048, 16))
ATT_BLOCK = 128
N_BUCKETS = 32
MAX_DISTANCE = 1024
MLSTM_HEADS = 4
MLSTM_HEAD_DIM = 128
MLSTM_WIDTH = MLSTM_HEADS * MLSTM_HEAD_DIM
MLSTM_CHUNK = 64
GLA_HEADS = 4
GLA_KEY_DIM = 64
GLA_VALUE_DIM = 128
GLA_KEY_WIDTH = GLA_HEADS * GLA_KEY_DIM
GLA_VALUE_WIDTH = GLA_HEADS * GLA_VALUE_DIM
GLA_GATE_RANK = 16
GLA_TAU = 16.0
GLA_CHUNK = 64
MIX_WIDTH = ATT_WIDTH + MLSTM_WIDTH + GLA_VALUE_WIDTH
N_EXPERTS = 16
EXPERT_FF = 2048
CAPACITY_FACTOR = 2
EPS = 1e-6

IN_SPLITS = (
    ('att_q', ATT_WIDTH), ('att_k', ATT_WIDTH), ('att_v', ATT_WIDTH),
    ('ml_q', MLSTM_WIDTH), ('ml_k', MLSTM_WIDTH), ('ml_v', MLSTM_WIDTH), ('ml_o', MLSTM_WIDTH),
    ('ml_gates', 4 * MLSTM_HEADS),
    ('gla_q', GLA_KEY_WIDTH), ('gla_k', GLA_KEY_WIDTH), ('gla_v', GLA_VALUE_WIDTH),
    ('gla_g', GLA_VALUE_WIDTH), ('gla_a', 2 * GLA_GATE_RANK),
)
IN_WIDTH = sum(w for _, w in IN_SPLITS)

kernel_name = 'hybrid_bidir_dilated_mlstm_gla_ec_encoder'


def split_columns(proj):
    out, off = {}, 0
    for name, w in IN_SPLITS:
        out[name] = proj[..., off:off + w]
        off += w
    return out


def rms_norm(x, gain):
    xf = x.astype(jnp.float32)
    y = xf * lax.rsqrt(jnp.mean(xf * xf, axis=-1, keepdims=True) + EPS)
    return (y * gain.astype(jnp.float32)).astype(x.dtype)


def t5_bucket(rel):
    half = N_BUCKETS // 2
    max_exact = half // 2
    n = np.abs(rel)
    large = max_exact + (np.log(np.maximum(n, 1) / max_exact) / math.log(MAX_DISTANCE / max_exact)
                         * (half - max_exact)).astype(np.int32)
    large = np.minimum(large, half - 1)
    return (np.where(rel > 0, half, 0) + np.where(n < max_exact, n, large)).astype(np.int32)


def banded_attention(q, k, v, rel_bias, half, dilation):
    g, L, h, dh = q.shape
    blk = math.gcd(L, ATT_BLOCK)
    nb = L // blk
    span = blk + 2 * half
    idx = np.arange(nb)[:, None] * blk + np.arange(span)[None, :]
    pad = ((0, 0), (half, half), (0, 0), (0, 0))
    kb = jnp.pad(k, pad)[:, idx]
    vb = jnp.pad(v, pad)[:, idx]
    qb = q.reshape(g, nb, blk, h, dh)
    rel = np.arange(span)[None, :] - half - np.arange(blk)[:, None]
    key_pos = idx - half
    valid = (np.abs(rel) <= half)[None] & ((key_pos >= 0) & (key_pos < L))[:, None, :]
    bias = jnp.transpose(rel_bias[t5_bucket(rel * dilation)], (2, 0, 1)).astype(jnp.float32)
    s = jnp.einsum('gnqhd,gnkhd->gnhqk', qb, kb, preferred_element_type=jnp.float32) * (dh ** -0.5) + bias
    s = jnp.where(valid[None, :, None], s, -jnp.inf)
    m = jnp.max(s, axis=-1)
    p = jnp.exp(s - m[..., None])
    l = jnp.sum(p, axis=-1)
    o = jnp.einsum('gnhqk,gnkhd->gnqhd', p, vb.astype(jnp.float32)) / jnp.transpose(l, (0, 1, 3, 2))[..., None]
    to_seq = lambda t: jnp.transpose(t, (0, 1, 3, 2)).reshape(g, L, h)
    return o.reshape(g, L, h, dh), to_seq(m), to_seq(l)


def dilated_attention(q, k, v, rel_bias):
    b, s, h, dh = q.shape
    outs, maxes, dens = [], [], []
    for window, d in DILATION_PAIRS:
        half = window // (2 * d)
        gather = lambda t: jnp.transpose(t.reshape(b, s // d, d, h, dh), (0, 2, 1, 3, 4)).reshape(b * d, s // d, h, dh)
        o, m, l = banded_attention(gather(q), gather(k), gather(v), rel_bias, half, d)
        outs.append(jnp.transpose(o.reshape(b, d, s // d, h, dh), (0, 2, 1, 3, 4)).reshape(b, s, h, dh))
        maxes.append(jnp.transpose(m.reshape(b, d, s // d, h), (0, 2, 1, 3)).reshape(b, s, h))
        dens.append(jnp.transpose(l.reshape(b, d, s // d, h), (0, 2, 1, 3)).reshape(b, s, h))
    m_all = jnp.stack(maxes)
    w = jnp.stack(dens) * jnp.exp(m_all - jnp.max(m_all, axis=0, keepdims=True))
    return jnp.einsum('rbsh,rbshd->bshd', w, jnp.stack(outs)) / jnp.sum(w, axis=0)[..., None]


def to_chunks(t, chunk):
    b, h, s = t.shape[:3]
    return jnp.moveaxis(t.reshape((b, h, s // chunk, chunk) + t.shape[3:]), 2, 0)


def from_chunks(t):
    nc, b, h, c = t.shape[:4]
    return jnp.moveaxis(t, 0, 2).reshape((b, h, nc * c) + t.shape[4:])


def flip_seq(t):
    return jnp.flip(t, axis=2)


def mlstm_scan(q, k, v, i_pre, log_f):
    b, h, s, dh = q.shape
    L = MLSTM_CHUNK
    causal = np.tril(np.ones((L, L), dtype=bool))

    def step(carry, inp):
        c_mat, n_vec, m_run = carry
        qc, kc, vc, ic, fc = inp
        cum = jnp.cumsum(fc, axis=-1)
        log_intra = jnp.where(causal, cum[..., :, None] - cum[..., None, :] + ic[..., None, :], -jnp.inf)
        log_inter = cum + m_run[..., None]
        m_t = jnp.maximum(log_inter, jnp.max(log_intra, axis=-1))
        w_inter = jnp.exp(log_inter - m_t)
        a = jnp.exp(log_intra - m_t[..., None]) * jnp.einsum('bhtd,bhsd->bhts', qc, kc)
        num = w_inter[..., None] * jnp.einsum('bhtd,bhde->bhte', qc, c_mat) + jnp.einsum('bhts,bhse->bhte', a, vc)
        den = w_inter * jnp.einsum('bhtd,bhd->bht', qc, n_vec) + jnp.sum(a, axis=-1)
        h_out = num / jnp.maximum(jnp.abs(den), jnp.exp(-m_t))[..., None]
        cum_end = cum[..., -1]
        log_w = cum_end[..., None] - cum + ic
        m_new = jnp.maximum(cum_end + m_run, jnp.max(log_w, axis=-1))
        w = jnp.exp(log_w - m_new[..., None])
        decay = jnp.exp(cum_end + m_run - m_new)
        c_new = decay[..., None, None] * c_mat + jnp.einsum('bhs,bhsd,bhse->bhde', w, kc, vc)
        n_new = decay[..., None] * n_vec + jnp.einsum('bhs,bhsd->bhd', w, kc)
        return (c_new, n_new, m_new), h_out

    init = (jnp.zeros((b, h, dh, dh), jnp.float32), jnp.zeros((b, h, dh), jnp.float32),
            jnp.zeros((b, h), jnp.float32))
    xs = tuple(to_chunks(t, L) for t in (q, k, v, i_pre, log_f))
    _, hs = lax.scan(step, init, xs)
    return from_chunks(hs)


def gla_scan(q, k, v, log_a):
    b, h, s, dk = q.shape
    dv = v.shape[-1]
    L = GLA_CHUNK
    causal = np.tril(np.ones((L, L), dtype=bool))[..., None]

    def step(state, inp):
        qc, kc, vc, ac = inp
        cum = jnp.cumsum(ac, axis=2)
        decay = jnp.exp(jnp.where(causal, cum[:, :, :, None, :] - cum[:, :, None, :, :], -jnp.inf))
        attn = jnp.einsum('bhtsd,bhsd->bhts', qc[:, :, :, None, :] * decay, kc)
        o = jnp.einsum('bhtd,bhde->bhte', qc * jnp.exp(cum), state) + jnp.einsum('bhts,bhse->bhte', attn, vc)
        cum_end = cum[:, :, -1]
        state = jnp.exp(cum_end)[..., None] * state + jnp.einsum(
            'bhsd,bhse->bhde', kc * jnp.exp(cum_end[:, :, None] - cum), vc)
        return state, o

    init = jnp.zeros((b, h, dk, dv), jnp.float32)
    xs = tuple(to_chunks(t, L) for t in (q, k, v, log_a))
    _, os_ = lax.scan(step, init, xs)
    return from_chunks(os_)


def token_mixer(hn, rel_bias, w_in, g_q_att, g_k_att, ml_gate_b, g_ml_out, gla_w_a, gla_b_a, g_gla_out, w_out):
    b, s, _ = hn.shape
    f32 = jnp.float32
    cols = split_columns(hn @ w_in)
    heads = lambda t, nh: jnp.transpose(t.reshape(b, s, nh, -1), (0, 2, 1, 3)).astype(f32)
    aq = rms_norm(cols['att_q'].reshape(b, s, ATT_HEADS, ATT_HEAD_DIM), g_q_att)
    ak = rms_norm(cols['att_k'].reshape(b, s, ATT_HEADS, ATT_HEAD_DIM), g_k_att)
    av = cols['att_v'].reshape(b, s, ATT_HEADS, ATT_HEAD_DIM)
    y_att = dilated_attention(aq, ak, av, rel_bias).reshape(b, s, ATT_WIDTH).astype(hn.dtype)
    mq = heads(cols['ml_q'], MLSTM_HEADS)
    mk = heads(cols['ml_k'], MLSTM_HEADS) * (MLSTM_HEAD_DIM ** -0.5)
    mv = heads(cols['ml_v'], MLSTM_HEADS)
    gates = jnp.transpose((cols['ml_gates'].astype(f32) + ml_gate_b.astype(f32)).reshape(b, s, 4, MLSTM_HEADS),
                          (2, 0, 3, 1))
    h_fwd = mlstm_scan(mq, mk, mv, gates[0], jax.nn.log_sigmoid(gates[1]))
    h_bwd = flip_seq(mlstm_scan(flip_seq(mq), flip_seq(mk), flip_seq(mv), flip_seq(gates[2]),
                                flip_seq(jax.nn.log_sigmoid(gates[3]))))
    h_ml = rms_norm(jnp.transpose(h_fwd + h_bwd, (0, 2, 1, 3)), g_ml_out.reshape(MLSTM_HEADS, MLSTM_HEAD_DIM))
    y_ml = (h_ml.reshape(b, s, MLSTM_WIDTH) * jax.nn.sigmoid(cols['ml_o'].astype(f32))).astype(hn.dtype)
    gq = heads(cols['gla_q'], GLA_HEADS) * (GLA_KEY_DIM ** -0.5)
    gk = heads(cols['gla_k'], GLA_HEADS)
    gv = heads(cols['gla_v'], GLA_HEADS)
    low = cols['gla_a'].reshape(b, s, 2, GLA_GATE_RANK)
    log_a = jax.nn.log_sigmoid(jnp.einsum('bsjr,jrk->jbsk', low, gla_w_a).astype(f32)
                               + gla_b_a[:, None, None, :].astype(f32)) / GLA_TAU
    log_a = jnp.transpose(log_a.reshape(2, b, s, GLA_HEADS, GLA_KEY_DIM), (0, 1, 3, 2, 4))
    o_fwd = gla_scan(gq, gk, gv, log_a[0])
    o_bwd = flip_seq(gla_scan(flip_seq(gq), flip_seq(gk), flip_seq(gv), flip_seq(log_a[1])))
    o_gla = rms_norm(jnp.transpose(o_fwd + o_bwd, (0, 2, 1, 3)), g_gla_out.reshape(GLA_HEADS, GLA_VALUE_DIM))
    y_gla = (o_gla.reshape(b, s, GLA_VALUE_WIDTH) * jax.nn.silu(cols['gla_g'].astype(f32))).astype(hn.dtype)
    y = jnp.concatenate([y_att, y_ml, y_gla], axis=-1)
    return y @ w_out


def expert_choice_ffn(hn, w_router, w_gate_e, w_up_e, w_down_e):
    b, s, dm = hn.shape
    t = b * s
    cap = CAPACITY_FACTOR * t // N_EXPERTS
    tokens = hn.reshape(t, dm)
    affinity = jax.nn.softmax((tokens @ w_router).astype(jnp.float32), axis=-1)
    gate, idx = lax.top_k(affinity.T, cap)
    xe = tokens[idx]
    hid = jax.nn.silu(jnp.einsum('ecd,edf->ecf', xe, w_gate_e)) * jnp.einsum('ecd,edf->ecf', xe, w_up_e)
    ye = jnp.einsum('ecf,efd->ecd', hid, w_down_e) * gate[..., None].astype(hn.dtype)
    y = jnp.zeros_like(tokens).at[idx.reshape(-1)].add(ye.reshape(-1, dm))
    return y.reshape(b, s, dm)


def encoder_layer(x, c, rel_bias, w_ada, b_ada, g_norm_mix, w_in, g_q_att, g_k_att, ml_gate_b, g_ml_out,
                  gla_w_a, gla_b_a, g_gla_out, w_out, g_norm_ffn, w_router, w_gate_e, w_up_e, w_down_e):
    mod = (jax.nn.silu(c) @ w_ada + b_ada)[:, None, :]
    shift1, scale1, gate1, shift2, scale2, gate2 = jnp.split(mod, 6, axis=-1)
    h = rms_norm(x, g_norm_mix) * (1 + scale1) + shift1
    x = x + gate1 * token_mixer(h, rel_bias, w_in, g_q_att, g_k_att, ml_gate_b, g_ml_out,
                                gla_w_a, gla_b_a, g_gla_out, w_out)
    h = rms_norm(x, g_norm_ffn) * (1 + scale2) + shift2
    x = x + gate2 * expert_choice_ffn(h, w_router, w_gate_e, w_up_e, w_down_e)
    return x


def setup_inputs(seed: int = 0) -> dict:
    key = jax.random.key(seed)
    ks = jax.random.split(key, 24)
    nrm = lambda k, shape, scale: jax.random.normal(k, shape, jnp.float32) * scale
    gain = lambda k, shape: 1.0 + 0.02 * jax.random.normal(k, shape, jnp.float32)
    d = D_MODEL
    i_bias = nrm(ks[7], (DEPTH, 2, MLSTM_HEADS), 0.1)
    f_bias = jnp.linspace(3.0, 6.0, MLSTM_HEADS, dtype=jnp.float32) + nrm(ks[8], (DEPTH, 2, MLSTM_HEADS), 0.1)
    ml_gate_b = jnp.stack([i_bias[:, 0], f_bias[:, 0], i_bias[:, 1], f_bias[:, 1]], axis=1).reshape(
        DEPTH, 4 * MLSTM_HEADS)
    return {
        'x_prompt': nrm(ks[0], (BATCH, SEQ, d), 1.0),
        'x_sample': nrm(ks[1], (DEC_BATCH, DEC_SEQ, d), 1.0),
        'c_prompt': nrm(ks[2], (BATCH, d), 1.0),
        'c_sample': nrm(ks[3], (DEC_BATCH, d), 1.0),
        'rel_bias': nrm(ks[4], (N_BUCKETS, ATT_HEADS), 0.5),
        'w_ada': nrm(ks[5], (DEPTH, d, 6 * d), 0.5 * d ** -0.5),
        'b_ada': nrm(ks[6], (DEPTH, 6 * d), 0.01),
        'g_norm_mix': gain(ks[9], (DEPTH, d)),
        'w_in': nrm(ks[10], (DEPTH, d, IN_WIDTH), d ** -0.5),
        'g_q_att': gain(ks[11], (DEPTH, ATT_HEAD_DIM)),
        'g_k_att': gain(ks[12], (DEPTH, ATT_HEAD_DIM)),
        'ml_gate_b': ml_gate_b,
        'g_ml_out': gain(ks[13], (DEPTH, MLSTM_WIDTH)),
        'gla_w_a': nrm(ks[14], (DEPTH, 2, GLA_GATE_RANK, GLA_KEY_WIDTH), GLA_GATE_RANK ** -0.5),
        'gla_b_a': nrm(ks[15], (DEPTH, 2, GLA_KEY_WIDTH), 0.01),
        'g_gla_out': gain(ks[16], (DEPTH, GLA_VALUE_WIDTH)),
        'w_out': nrm(ks[17], (DEPTH, MIX_WIDTH, d), MIX_WIDTH ** -0.5),
        'g_norm_ffn': gain(ks[18], (DEPTH, d)),
        'w_router': nrm(ks[19], (DEPTH, d, N_EXPERTS), d ** -0.5),
        'w_gate_e': nrm(ks[20], (DEPTH, N_EXPERTS, d, EXPERT_FF), d ** -0.5),
        'w_up_e': nrm(ks[21], (DEPTH, N_EXPERTS, d, EXPERT_FF), d ** -0.5),
        'w_down_e': nrm(ks[22], (DEPTH, N_EXPERTS, EXPERT_FF, d), EXPERT_FF ** -0.5),
    }


def reference(x_prompt, x_sample, c_prompt, c_sample, rel_bias, w_ada, b_ada, g_norm_mix, w_in, g_q_att,
              g_k_att, ml_gate_b, g_ml_out, gla_w_a, gla_b_a, g_gla_out, w_out, g_norm_ffn, w_router,
              w_gate_e, w_up_e, w_down_e):
    def trunk(x, c):
        for l in range(DEPTH):
            x = encoder_layer(x, c, rel_bias, w_ada[l], b_ada[l], g_norm_mix[l], w_in[l], g_q_att[l],
                              g_k_att[l], ml_gate_b[l], g_ml_out[l], gla_w_a[l], gla_b_a[l], g_gla_out[l],
                              w_out[l], g_norm_ffn[l], w_router[l], w_gate_e[l], w_up_e[l], w_down_e[l])
        return x
    y_prompt = trunk(x_prompt, c_prompt)
    y_sample = trunk(x_sample, c_sample)
    return (y_prompt, y_sample)
```

```python
import functools
import math

import numpy as np
import jax
import jax.numpy as jnp
from jax import lax
from jax.experimental import pallas as pl
from jax.experimental.pallas import tpu as pltpu

F32 = jnp.float32
BF16 = jnp.bfloat16
HI = lax.Precision.HIGHEST

D_MODEL = 2048
DEPTH = 2
ATT_HEADS = 8
HEAD_DIM = 128
DILATIONS = (1, 4, 16)
ATT_HALF = 64
N_BUCKETS = 32
MAX_DISTANCE = 1024
ML_HEADS = 4
ML_CHUNK = 128
GLA_HEADS = 4
GLA_DK = 64
GLA_DV = 128
GLA_RANK = 16
GLA_TAU = 16.0
GLA_CHUNK = 64
N_EXPERTS = 16
EXPERT_FF = 2048
CAPACITY_FACTOR = 2
EPS = 1e-6
NEG = -1e30

W_MAIN = 6656
COL_ML_Q, COL_ML_K, COL_ML_V, COL_ML_O = 3072, 3584, 4096, 4608
COL_GLA_Q, COL_GLA_K, COL_GLA_V, COL_GLA_G = 5120, 5376, 5632, 6144
W_SMALL = 128
PROJ_TN = 512

VMEM_LIMIT = 56 * 1024 * 1024


def _params(sem):
    return pltpu.CompilerParams(dimension_semantics=sem, vmem_limit_bytes=VMEM_LIMIT)


def _logsigmoid(x):
    return jnp.minimum(x, 0.0) - jnp.log1p(jnp.exp(-jnp.abs(x)))


def _nt(a, b):
    return lax.dot_general(a, b, (((1,), (1,)), ((), ())), preferred_element_type=F32)


def _tn(a, b):
    return lax.dot_general(a, b, (((0,), (0,)), ((), ())), preferred_element_type=F32)


def _ada_kernel(c_ref, w_ref, b_ref, o_ref):
    c = c_ref[...]
    a = c * jax.nn.sigmoid(c)
    o_ref[...] = jnp.dot(a, w_ref[...], precision=HI, preferred_element_type=F32) + b_ref[...]


def _ada_call(c_all, w_ada, b_ada):
    bp = c_all.shape[0]
    tn = 512
    n = 6 * D_MODEL
    return pl.pallas_call(
        _ada_kernel,
        grid=(DEPTH, n // tn),
        in_specs=[pl.BlockSpec((bp, D_MODEL), lambda l, j: (0, 0)),
                  pl.BlockSpec((None, D_MODEL, tn), lambda l, j: (l, 0, j)),
                  pl.BlockSpec((None, 1, tn), lambda l, j: (l, 0, j))],
        out_specs=pl.BlockSpec((None, bp, tn), lambda l, j: (l, 0, j)),
        out_shape=jax.ShapeDtypeStruct((DEPTH, bp, n), F32),
        compiler_params=_params(("arbitrary", "arbitrary")),
        name="adaln",
    )(c_all, w_ada, b_ada.reshape(DEPTH, 1, n))


def _inproj_kernel(x_ref, mod_ref, g_ref, w_ref, ws_ref, gq_ref, gk_ref, o_ref, os_ref, h_scr):
    j = pl.program_id(1)

    @pl.when(j == 0)
    def _():
        x = x_ref[...]
        y = x * lax.rsqrt(jnp.mean(x * x, axis=-1, keepdims=True) + EPS) * g_ref[...]
        h = (y * (1.0 + mod_ref[1:2, :]) + mod_ref[0:1, :]).astype(BF16)
        h_scr[...] = h
        os_ref[...] = jnp.dot(h, ws_ref[...], preferred_element_type=F32)

    acc = jnp.dot(h_scr[...], w_ref[...], preferred_element_type=F32)

    def head_norm(gain, scale):
        for hh in range(PROJ_TN // HEAD_DIM):
            a = acc[:, hh * HEAD_DIM:(hh + 1) * HEAD_DIM]
            r = lax.rsqrt(jnp.mean(a * a, axis=-1, keepdims=True) + EPS)
            o_ref[:, hh * HEAD_DIM:(hh + 1) * HEAD_DIM] = a * r * (gain * scale)

    @pl.when(j < 2)
    def _():
        head_norm(gq_ref[...], HEAD_DIM ** -0.5)

    @pl.when(jnp.logical_and(j >= 2, j < 4))
    def _():
        head_norm(gk_ref[...], 1.0)

    @pl.when(j >= 4)
    def _():
        o_ref[...] = acc


def _inproj_call(x2d, mod, g_norm, w_main, w_small, g_q, g_k, seq):
    t = x2d.shape[0]
    tm = min(1024, seq)
    return pl.pallas_call(
        _inproj_kernel,
        grid=(t // tm, W_MAIN // PROJ_TN),
        in_specs=[pl.BlockSpec((tm, D_MODEL), lambda i, j: (i, 0)),
                  pl.BlockSpec((None, 8, D_MODEL), lambda i, j: ((i * tm) // seq, 0, 0)),
                  pl.BlockSpec((1, D_MODEL), lambda i, j: (0, 0)),
                  pl.BlockSpec((D_MODEL, PROJ_TN), lambda i, j: (0, j)),
                  pl.BlockSpec((D_MODEL, W_SMALL), lambda i, j: (0, 0)),
                  pl.BlockSpec((1, HEAD_DIM), lambda i, j: (0, 0)),
                  pl.BlockSpec((1, HEAD_DIM), lambda i, j: (0, 0))],
        out_specs=[pl.BlockSpec((tm, PROJ_TN), lambda i, j: (i, j)),
                   pl.BlockSpec((tm, W_SMALL), lambda i, j: (i, 0))],
        out_shape=[jax.ShapeDtypeStruct((t, W_MAIN), F32),
                   jax.ShapeDtypeStruct((t, W_SMALL), F32)],
        scratch_shapes=[pltpu.VMEM((tm, D_MODEL), BF16)],
        compiler_params=_params(("arbitrary", "arbitrary")),
        name="inproj",
    )(x2d, mod, g_norm, w_main, w_small, g_q, g_k)


def _attn_kernel(q_ref, kp_ref, kc_ref, kn_ref, vp_ref, vc_ref, vn_ref, bias_ref, o_ref, st_ref,
                 kcat, vcat, *, tq, sub_len):
    ib = pl.program_id(1)
    h0, h1 = ATT_HALF, ATT_HALF + tq
    kcat[0:h0, :] = kp_ref[...].astype(BF16)
    kcat[h0:h1, :] = kc_ref[...].astype(BF16)
    kcat[h1:h1 + ATT_HALF, :] = kn_ref[...].astype(BF16)
    vcat[0:h0, :] = vp_ref[...].astype(BF16)
    vcat[h0:h1, :] = vc_ref[...].astype(BF16)
    vcat[h1:h1 + ATT_HALF, :] = vn_ref[...].astype(BF16)
    lane = lax.broadcasted_iota(jnp.int32, (128, 128), 1)
    col = lax.broadcasted_iota(jnp.int32, (1, 256), 1)
    for sb in range(tq // 128):
        kpos = ib * tq + (sb * 128 - ATT_HALF) + col
        pen = jnp.where(jnp.logical_and(kpos >= 0, kpos < sub_len), 0.0, NEG)
        slab = jnp.zeros((128, 128), F32)
        for hh in range(4):
            cs = slice(hh * HEAD_DIM, (hh + 1) * HEAD_DIM)
            q = q_ref[sb * 128:(sb + 1) * 128, cs].astype(BF16)
            k = kcat[sb * 128:sb * 128 + 256, cs]
            v = vcat[sb * 128:sb * 128 + 256, cs]
            s = _nt(q, k) + bias_ref[hh] + pen
            m = jnp.max(s, axis=-1, keepdims=True)
            p = jnp.exp(s - m)
            l = jnp.sum(p, axis=-1, keepdims=True)
            o_ref[sb * 128:(sb + 1) * 128, cs] = jnp.dot(p.astype(BF16), v, preferred_element_type=F32)
            slab = jnp.where(lane == hh, m, slab)
            slab = jnp.where(lane == 4 + hh, l, slab)
        st_ref[sb * 128:(sb + 1) * 128, :] = slab


def _attn_branch_call(proj, bias, batch, seq, dil):
    t = proj.shape[0]
    sub_len = seq // dil
    tq = min(512, sub_len)
    nblk = sub_len // tq
    rows = t // dil
    nwb = W_MAIN // 512
    view = proj.reshape(rows, dil * W_MAIN)
    last64 = rows // ATT_HALF - 1
    r64 = tq // ATT_HALF

    def row(b, ib):
        return b * nblk + ib

    def prev_idx(b, ib):
        return jnp.maximum(row(b, ib) * r64 - 1, 0)

    def next_idx(b, ib):
        return jnp.minimum((row(b, ib) + 1) * r64, last64)

    def cur(off):
        return pl.BlockSpec((tq, 512), lambda b, ib, r, g: (row(b, ib), r * nwb + off + g))

    def prev(off):
        return pl.BlockSpec((ATT_HALF, 512), lambda b, ib, r, g: (prev_idx(b, ib), r * nwb + off + g))

    def nxt(off):
        return pl.BlockSpec((ATT_HALF, 512), lambda b, ib, r, g: (next_idx(b, ib), r * nwb + off + g))

    o, st = pl.pallas_call(
        functools.partial(_attn_kernel, tq=tq, sub_len=sub_len),
        grid=(batch, nblk, dil, 2),
        in_specs=[cur(0), prev(2), cur(2), nxt(2), prev(4), cur(4), nxt(4),
                  pl.BlockSpec((4, 128, 256), lambda b, ib, r, g: (g, 0, 0))],
        out_specs=[pl.BlockSpec((tq, 512), lambda b, ib, r, g: (row(b, ib), r * 2 + g)),
                   pl.BlockSpec((tq, 128), lambda b, ib, r, g: (row(b, ib), r * 2 + g))],
        out_shape=[jax.ShapeDtypeStruct((rows, dil * 1024), F32),
                   jax.ShapeDtypeStruct((rows, dil * 256), F32)],
        scratch_shapes=[pltpu.VMEM((tq + 2 * ATT_HALF, 512), BF16),
                        pltpu.VMEM((tq + 2 * ATT_HALF, 512), BF16)],
        compiler_params=_params(("arbitrary",) * 4),
        name="attn_d%d" % dil,
    )(view, view, view, view, view, view, view, bias)
    return o.reshape(t, 1024), st.reshape(t, 256)


def _t5_bucket(rel):
    half = N_BUCKETS // 2
    max_exact = half // 2
    n = np.abs(rel)
    large = max_exact + (np.log(np.maximum(n, 1) / max_exact) / math.log(MAX_DISTANCE / max_exact)
                         * (half - max_exact)).astype(np.int32)
    large = np.minimum(large, half - 1)
    return (np.where(rel > 0, half, 0) + np.where(n < max_exact, n, large)).astype(np.int32)


def _attn_bias(rel_bias, dil):
    rel = np.arange(256)[None, :] - ATT_HALF - np.arange(128)[:, None]
    table = jnp.transpose(rel_bias[_t5_bucket(rel * dil)], (2, 0, 1)).astype(F32)
    return jnp.where(jnp.asarray(np.abs(rel) <= ATT_HALF)[None], table, NEG)


def _order_mask(n, direction):
    ti = lax.broadcasted_iota(jnp.int32, (n, n), 0)
    si = lax.broadcasted_iota(jnp.int32, (n, n), 1)
    return (si - ti) * (1 - 2 * direction) <= 0


def _mlstm_kernel(q_ref, k_ref, v_ref, g_ref, gb_ref, o_ref, c_scr, m_scr, *, ncb):
    dr = pl.program_id(0)
    L = ML_CHUNK

    @pl.when(pl.program_id(2) == 0)
    def _():
        c_scr[...] = jnp.zeros_like(c_scr)
        m_scr[...] = jnp.zeros_like(m_scr)

    mask = _order_mask(L, dr)
    tri = jnp.where(mask, 1.0, 0.0)
    lane = lax.broadcasted_iota(jnp.int32, (L, 128), 1)
    ones = jnp.ones((L, HEAD_DIM), BF16)

    def body(jj, carry):
        j = jnp.where(dr == 0, jj, ncb - 1 - jj)
        st = pl.multiple_of(j * L, L)
        g = g_ref[pl.ds(st, L), :] + gb_ref[...]
        g = jnp.where(dr == 0, g, pltpu.roll(g, 128 - 2 * ML_HEADS, 1))
        lf = _logsigmoid(g)
        cum = jnp.dot(tri, lf, precision=HI, preferred_element_type=F32)
        xc = jnp.where(lane < ML_HEADS, g, cum)
        xr = xc.T
        tot = jnp.sum(lf, axis=0, keepdims=True)
        for h in range(ML_HEADS):
            cs = slice(h * HEAD_DIM, (h + 1) * HEAD_DIM)
            i_col, cf_col = xc[:, h:h + 1], xc[:, ML_HEADS + h:ML_HEADS + h + 1]
            i_row, cf_row = xr[h:h + 1, :], xr[ML_HEADS + h:ML_HEADS + h + 1, :]
            tot_h = tot[:, ML_HEADS + h:ML_HEADS + h + 1]
            m_run = m_scr[h][0:1, 0:1]
            log_intra = jnp.where(mask, cf_col - cf_row + i_row, NEG)
            log_inter = cf_col + m_run
            m_t = jnp.maximum(log_inter, jnp.max(log_intra, axis=-1, keepdims=True))
            w_inter = jnp.exp(log_inter - m_t)
            qh = q_ref[pl.ds(st, L), cs].astype(BF16)
            kf = k_ref[pl.ds(st, L), cs] * (HEAD_DIM ** -0.5)
            vaug = jnp.concatenate([v_ref[pl.ds(st, L), cs].astype(BF16), ones], axis=1)
            a = jnp.exp(log_intra - m_t) * _nt(qh, kf.astype(BF16))
            c_prev = c_scr[h]
            nd = (w_inter * jnp.dot(qh, c_prev.astype(BF16), preferred_element_type=F32)
                  + jnp.dot(a.astype(BF16), vaug, preferred_element_type=F32))
            den = nd[:, HEAD_DIM:HEAD_DIM + 1]
            o_ref[pl.ds(st, L), cs] = nd[:, :HEAD_DIM] / jnp.maximum(jnp.abs(den), jnp.exp(-m_t))
            lw = tot_h - cf_col + i_col
            m_new = jnp.maximum(tot_h + m_run, jnp.max(lw, axis=0, keepdims=True))
            kw = (kf * jnp.exp(lw - m_new)).astype(BF16)
            c_scr[h] = jnp.exp(tot_h + m_run - m_new) * c_prev + _tn(kw, vaug)
            m_scr[h] = jnp.broadcast_to(m_new, (8, 128))
        return carry

    lax.fori_loop(0, ncb, body, 0)


def _mlstm_call(proj, small, gate_bias, batch, seq):
    t = proj.shape[0]
    tb = min(1024, seq)
    nblk = seq // tb

    def blk(d, b, c):
        return b * nblk + jnp.where(d == 0, c, nblk - 1 - c)

    def col(off):
        return pl.BlockSpec((tb, 512), lambda d, b, c: (blk(d, b, c), off // 512))

    return pl.pallas_call(
        functools.partial(_mlstm_kernel, ncb=tb // ML_CHUNK),
        grid=(2, batch, nblk),
        in_specs=[col(COL_ML_Q), col(COL_ML_K), col(COL_ML_V),
                  pl.BlockSpec((tb, W_SMALL), lambda d, b, c: (blk(d, b, c), 0)),
                  pl.BlockSpec((1, W_SMALL), lambda d, b, c: (0, 0))],
        out_specs=pl.BlockSpec((None, tb, 512), lambda d, b, c: (d, blk(d, b, c), 0)),
        out_shape=jax.ShapeDtypeStruct((2, t, 512), F32),
        scratch_shapes=[pltpu.VMEM((ML_HEADS, HEAD_DIM, 2 * HEAD_DIM), F32),
                        pltpu.VMEM((ML_HEADS, 8, 128), F32)],
        compiler_params=_params(("arbitrary",) * 3),
        name="mlstm",
    )(proj, proj, proj, small, gate_bias)


def _gla_kernel(q_ref, k_ref, v_ref, sm_ref, wa_ref, ba_ref, o_ref, s_scr, *, ncb):
    dr = pl.program_id(0)
    L = GLA_CHUNK

    @pl.when(pl.program_id(2) == 0)
    def _():
        s_scr[...] = jnp.zeros_like(s_scr)

    mask = _order_mask(L, dr)
    tri = jnp.where(mask, 1.0, 0.0)

    def body(jj, carry):
        j = jnp.where(dr == 0, jj, ncb - 1 - jj)
        st = pl.multiple_of(j * L, L)
        x = jnp.dot(sm_ref[pl.ds(st, L), :], wa_ref[...], precision=HI, preferred_element_type=F32) + ba_ref[...]
        la = _logsigmoid(x) * (1.0 / GLA_TAU)
        cum = jnp.dot(tri, la, precision=HI, preferred_element_type=F32)
        tot = jnp.sum(la, axis=0, keepdims=True)
        mid = cum[L // 2:L // 2 + 1, :]
        q = q_ref[pl.ds(st, L), :] * (GLA_DK ** -0.5)
        k = k_ref[pl.ds(st, L), :]
        q_in = (q * jnp.exp(cum - mid)).astype(BF16)
        k_in = (k * jnp.exp(mid - cum)).astype(BF16)
        q_st = (q * jnp.exp(cum)).astype(BF16)
        k_st = (k * jnp.exp(tot - cum)).astype(BF16)
        decay = jnp.exp(tot)
        for h in range(GLA_HEADS):
            ks = slice(h * GLA_DK, (h + 1) * GLA_DK)
            vs = slice(h * GLA_DV, (h + 1) * GLA_DV)
            attn = jnp.where(mask, _nt(q_in[:, ks], k_in[:, ks]), 0.0)
            vh = v_ref[pl.ds(st, L), vs].astype(BF16)
            s_prev = s_scr[h]
            o_ref[pl.ds(st, L), vs] = (_nt(q_st[:, ks], s_prev.astype(BF16))
                                       + jnp.dot(attn.astype(BF16), vh, preferred_element_type=F32))
            s_scr[h] = decay[:, ks] * s_prev + _tn(vh, k_st[:, ks])
        return carry

    lax.fori_loop(0, ncb, body, 0)


def _gla_call(proj, small, wa_pad, ba, batch, seq):
    t = proj.shape[0]
    tb = min(1024, seq)
    nblk = seq // tb

    def blk(d, b, c):
        return b * nblk + jnp.where(d == 0, c, nblk - 1 - c)

    return pl.pallas_call(
        functools.partial(_gla_kernel, ncb=tb // GLA_CHUNK),
        grid=(2, batch, nblk),
        in_specs=[pl.BlockSpec((tb, 256), lambda d, b, c: (blk(d, b, c), COL_GLA_Q // 256)),
                  pl.BlockSpec((tb, 256), lambda d, b, c: (blk(d, b, c), COL_GLA_K // 256)),
                  pl.BlockSpec((tb, 512), lambda d, b, c: (blk(d, b, c), COL_GLA_V // 512)),
                  pl.BlockSpec((tb, W_SMALL), lambda d, b, c: (blk(d, b, c), 0)),
                  pl.BlockSpec((None, W_SMALL, 256), lambda d, b, c: (d, 0, 0)),
                  pl.BlockSpec((None, 1, 256), lambda d, b, c: (d, 0, 0))],
        out_specs=pl.BlockSpec((None, tb, 512), lambda d, b, c: (d, blk(d, b, c), 0)),
        out_shape=jax.ShapeDtypeStruct((2, t, 512), F32),
        scratch_shapes=[pltpu.VMEM((GLA_HEADS, GLA_DV, GLA_DK), F32)],
        compiler_params=_params(("arbitrary",) * 3),
        name="gla",
    )(proj, proj, proj, small, wa_pad, ba)


def _merge_kernel(o1_ref, o4_ref, o16_ref, s1_ref, s4_ref, s16_ref, mf_ref, mb_ref, mo_ref,
                  gf_ref, gb_ref, gg_ref, x_ref, mod_ref, gml_ref, ggl_ref, w_ref, out_ref, y_scr):
    o_refs = (o1_ref, o4_ref, o16_ref)
    s_refs = (s1_ref, s4_ref, s16_ref)
    for h in range(ATT_HEADS):
        cs = slice(h * HEAD_DIM, (h + 1) * HEAD_DIM)
        lm = (h // 4) * 128 + (h % 4)
        ms = [s[:, lm:lm + 1] for s in s_refs]
        ls = [s[:, lm + 4:lm + 5] for s in s_refs]
        m_all = jnp.maximum(jnp.maximum(ms[0], ms[1]), ms[2])
        ws = [jnp.exp(m - m_all) for m in ms]
        num = ws[0] * o_refs[0][:, cs] + ws[1] * o_refs[1][:, cs] + ws[2] * o_refs[2][:, cs]
        den = ws[0] * ls[0] + ws[1] * ls[1] + ws[2] * ls[2]
        y_scr[:, cs] = (num / den).astype(BF16)
    for h in range(ML_HEADS):
        cs = slice(h * HEAD_DIM, (h + 1) * HEAD_DIM)
        a = mf_ref[:, cs] + mb_ref[:, cs]
        a = a * lax.rsqrt(jnp.mean(a * a, axis=-1, keepdims=True) + EPS) * gml_ref[:, cs]
        y_scr[:, 1024 + h * HEAD_DIM:1024 + (h + 1) * HEAD_DIM] = (a * jax.nn.sigmoid(mo_ref[:, cs])).astype(BF16)
    for h in range(GLA_HEADS):
        cs = slice(h * GLA_DV, (h + 1) * GLA_DV)
        a = gf_ref[:, cs] + gb_ref[:, cs]
        a = a * lax.rsqrt(jnp.mean(a * a, axis=-1, keepdims=True) + EPS) * ggl_ref[:, cs]
        gg = gg_ref[:, cs]
        y_scr[:, 1536 + h * GLA_DV:1536 + (h + 1) * GLA_DV] = (a * (gg * jax.nn.sigmoid(gg))).astype(BF16)
    mix = jnp.dot(y_scr[...], w_ref[...], preferred_element_type=F32)
    out_ref[...] = x_ref[...] + mod_ref[2:3, :] * mix


def _merge_call(att_o, att_s, h_ml, o_gla, proj, x2d, mod, g_ml, g_gla, w_out, seq):
    t = x2d.shape[0]
    tm = 256
    row = lambda w: pl.BlockSpec((tm, w), lambda i: (i, 0))
    dirblk = lambda d: pl.BlockSpec((None, tm, 512), lambda i: (d, i, 0))
    return pl.pallas_call(
        _merge_kernel,
        grid=(t // tm,),
        in_specs=[row(1024), row(1024), row(1024), row(256), row(256), row(256),
                  dirblk(0), dirblk(1), pl.BlockSpec((tm, 512), lambda i: (i, COL_ML_O // 512)),
                  dirblk(0), dirblk(1), pl.BlockSpec((tm, 512), lambda i: (i, COL_GLA_G // 512)),
                  row(D_MODEL),
                  pl.BlockSpec((None, 8, D_MODEL), lambda i: ((i * tm) // seq, 0, 0)),
                  pl.BlockSpec((1, 512), lambda i: (0, 0)),
                  pl.BlockSpec((1, 512), lambda i: (0, 0)),
                  pl.BlockSpec((D_MODEL, D_MODEL), lambda i: (0, 0))],
        out_specs=row(D_MODEL),
        out_shape=jax.ShapeDtypeStruct((t, D_MODEL), F32),
        scratch_shapes=[pltpu.VMEM((tm, D_MODEL), BF16)],
        compiler_params=_params(("arbitrary",)),
        name="merge_outproj",
    )(att_o[0], att_o[1], att_o[2], att_s[0], att_s[1], att_s[2], h_ml, h_ml, proj,
      o_gla, o_gla, proj, x2d, mod, g_ml, g_gla, w_out)


def _router_kernel(x_ref, mod_ref, g_ref, wr_ref, h_ref, aff_ref):
    x = x_ref[...]
    y = x * lax.rsqrt(jnp.mean(x * x, axis=-1, keepdims=True) + EPS) * g_ref[...]
    h = y * (1.0 + mod_ref[4:5, :]) + mod_ref[3:4, :]
    h_ref[...] = h
    logits = lax.dot_general(wr_ref[...], h, (((1,), (1,)), ((), ())), precision=HI,
                             preferred_element_type=F32)
    e = jnp.exp(logits - jnp.max(logits, axis=0, keepdims=True))
    aff_ref[...] = e / jnp.sum(e, axis=0, keepdims=True)


def _router_call(x2d, mod, g_norm, w_router_t, seq):
    t = x2d.shape[0]
    tm = 512
    return pl.pallas_call(
        _router_kernel,
        grid=(t // tm,),
        in_specs=[pl.BlockSpec((tm, D_MODEL), lambda i: (i, 0)),
                  pl.BlockSpec((None, 8, D_MODEL), lambda i: ((i * tm) // seq, 0, 0)),
                  pl.BlockSpec((1, D_MODEL), lambda i: (0, 0)),
                  pl.BlockSpec((N_EXPERTS, D_MODEL), lambda i: (0, 0))],
        out_specs=[pl.BlockSpec((tm, D_MODEL), lambda i: (i, 0)),
                   pl.BlockSpec((N_EXPERTS, tm), lambda i: (0, i))],
        out_shape=[jax.ShapeDtypeStruct((t, D_MODEL), F32),
                   jax.ShapeDtypeStruct((N_EXPERTS, t), F32)],
        compiler_params=_params(("arbitrary",)),
        name="norm_router",
    )(x2d, mod, g_norm, w_router_t)


FF_CHUNK = 512
SEQ_SLOTS = 128


def _expert_kernel(idx_ref, gate_ref, seq_ref, g2_ref, wg_ref, wu_ref, wd_ref, h_hbm, yin_hbm, y_hbm,
                   xbuf, xb, acc, ybuf, sems, *, tm, cap):
    del yin_hbm
    e, i, f = pl.program_id(0), pl.program_id(1), pl.program_id(2)
    base = e * cap + i * tm

    def row_copy(src, dst, sem, r, to_hbm):
        t = idx_ref[base + r]
        if to_hbm:
            return pltpu.make_async_copy(src.at[pl.ds(r, 1)], dst.at[pl.ds(t, 1)], sem)
        return pltpu.make_async_copy(src.at[pl.ds(t, 1)], dst.at[pl.ds(r, 1)], sem)

    def run_rows(src, dst, sem, to_hbm):
        def start(r, c):
            row_copy(src, dst, sem, r, to_hbm).start()
            return c

        def wait(r, c):
            row_copy(src, dst, sem, r, to_hbm).wait()
            return c

        lax.fori_loop(0, tm, start, 0)
        lax.fori_loop(0, tm, wait, 0)

    @pl.when(f == 0)
    def _():
        run_rows(h_hbm, xbuf, sems.at[0], False)
        xb[...] = xbuf[...].astype(BF16)
        acc[...] = jnp.zeros_like(acc)

    x = xb[...]
    g = jnp.dot(x, wg_ref[...], preferred_element_type=F32)
    u = jnp.dot(x, wu_ref[...], preferred_element_type=F32)
    hid = (g * jax.nn.sigmoid(g) * u).astype(BF16)
    acc[...] += jnp.dot(hid, wd_ref[...], preferred_element_type=F32)

    @pl.when(f == pl.num_programs(2) - 1)
    def _():
        run_rows(y_hbm, ybuf, sems.at[1], False)
        slot = lax.broadcasted_iota(jnp.int32, (tm, SEQ_SLOTS), 1)
        onehot = jnp.where((slot & 31) == seq_ref[...], 1.0, 0.0).astype(BF16)
        scale = jnp.dot(onehot, g2_ref[...], preferred_element_type=F32)
        ybuf[...] = ybuf[...] + acc[...] * gate_ref[...] * scale
        run_rows(ybuf, y_hbm, sems.at[2], True)


def _expert_call(h2, x1, idx, gate, g2_limbs, wg, wu, wd, seq):
    t = h2.shape[0]
    cap = idx.shape[1]
    tm = min(512, cap)
    nt = cap // tm
    idx_flat = idx.reshape(-1).astype(jnp.int32)
    gate_col = gate.reshape(N_EXPERTS * nt, tm, 1)
    seq_col = (idx_flat // seq).reshape(N_EXPERTS * nt, tm, 1)
    grid_spec = pltpu.PrefetchScalarGridSpec(
        num_scalar_prefetch=1,
        grid=(N_EXPERTS, nt, EXPERT_FF // FF_CHUNK),
        in_specs=[pl.BlockSpec((None, tm, 1), lambda e, i, f, idx: (e * nt + i, 0, 0)),
                  pl.BlockSpec((None, tm, 1), lambda e, i, f, idx: (e * nt + i, 0, 0)),
                  pl.BlockSpec((SEQ_SLOTS, D_MODEL), lambda e, i, f, idx: (0, 0)),
                  pl.BlockSpec((None, D_MODEL, FF_CHUNK), lambda e, i, f, idx: (e, 0, f)),
                  pl.BlockSpec((None, D_MODEL, FF_CHUNK), lambda e, i, f, idx: (e, 0, f)),
                  pl.BlockSpec((None, FF_CHUNK, D_MODEL), lambda e, i, f, idx: (e, f, 0)),
                  pl.BlockSpec(memory_space=pl.ANY),
                  pl.BlockSpec(memory_space=pl.ANY)],
        out_specs=pl.BlockSpec(memory_space=pl.ANY),
        scratch_shapes=[pltpu.VMEM((tm, D_MODEL), F32),
                        pltpu.VMEM((tm, D_MODEL), BF16),
                        pltpu.VMEM((tm, D_MODEL), F32),
                        pltpu.VMEM((tm, D_MODEL), F32),
                        pltpu.SemaphoreType.DMA((3,))],
    )
    return pl.pallas_call(
        functools.partial(_expert_kernel, tm=tm, cap=cap),
        grid_spec=grid_spec,
        out_shape=jax.ShapeDtypeStruct((t, D_MODEL), F32),
        input_output_aliases={8: 0},
        compiler_params=_params(("arbitrary",) * 3),
        name="experts",
    )(idx_flat, gate_col, seq_col, g2_limbs, wg, wu, wd, h2, x1)


def _bf16_limbs(g2):
    b = g2.shape[0]
    hi = g2.astype(BF16)
    r1 = g2 - hi.astype(F32)
    mid = r1.astype(BF16)
    lo = (r1 - mid.astype(F32)).astype(BF16)
    out = jnp.zeros((SEQ_SLOTS, D_MODEL), BF16)
    for k, limb in enumerate((hi, mid, lo)):
        out = out.at[32 * k:32 * k + b].set(limb)
    return out


def _prep_layer(l, w_in, g_q_att, g_k_att, ml_gate_b, g_ml_out, gla_w_a, gla_b_a, g_gla_out, w_out,
                g_norm_mix, g_norm_ffn, w_router, w_gate_e, w_up_e, w_down_e):
    wi = w_in[l]
    w_main = jnp.concatenate([wi[:, :5120], wi[:, 5136:6672]], axis=1).astype(BF16)
    w_small = jnp.concatenate([wi[:, 5120:5136], wi[:, 6672:6704],
                               jnp.zeros((D_MODEL, W_SMALL - 48), F32)], axis=1).astype(BF16)
    gate_bias = jnp.zeros((1, W_SMALL), F32).at[0, :16].set(ml_gate_b[l])
    wa_pad = jnp.zeros((2, W_SMALL, GLA_HEADS * GLA_DK), F32)
    wa_pad = wa_pad.at[0, 16:32].set(gla_w_a[l, 0]).at[1, 32:48].set(gla_w_a[l, 1])
    return dict(
        w_main=w_main, w_small=w_small, gate_bias=gate_bias, wa_pad=wa_pad,
        ba=gla_b_a[l].reshape(2, 1, -1), g_q=g_q_att[l].reshape(1, -1), g_k=g_k_att[l].reshape(1, -1),
        g_ml=g_ml_out[l].reshape(1, -1), g_gla=g_gla_out[l].reshape(1, -1), w_out=w_out[l].astype(BF16),
        g_mix=g_norm_mix[l].reshape(1, -1), g_ffn=g_norm_ffn[l].reshape(1, -1),
        w_router_t=w_router[l].T, wg=w_gate_e[l].astype(BF16), wu=w_up_e[l].astype(BF16),
        wd=w_down_e[l].astype(BF16))


def _layer(x2d, mod, biases, p, batch, seq):
    t = x2d.shape[0]
    proj, small = _inproj_call(x2d, mod, p["g_mix"], p["w_main"], p["w_small"], p["g_q"], p["g_k"], seq)
    att_o, att_s = [], []
    for dil, bias in zip(DILATIONS, biases):
        o, s = _attn_branch_call(proj, bias, batch, seq, dil)
        att_o.append(o)
        att_s.append(s)
    h_ml = _mlstm_call(proj, small, p["gate_bias"], batch, seq)
    o_gla = _gla_call(proj, small, p["wa_pad"], p["ba"], batch, seq)
    x1 = _merge_call(att_o, att_s, h_ml, o_gla, proj, x2d, mod, p["g_ml"], p["g_gla"], p["w_out"], seq)
    h2, aff = _router_call(x1, mod, p["g_ffn"], p["w_router_t"], seq)
    cap = CAPACITY_FACTOR * t // N_EXPERTS
    gate, idx = lax.top_k(aff, cap)
    return _expert_call(h2, x1, idx, gate, _bf16_limbs(mod[:, 5, :]), p["wg"], p["wu"], p["wd"], seq)


def kernel(x_prompt, x_sample, c_prompt, c_sample, rel_bias, w_ada, b_ada, g_norm_mix, w_in, g_q_att, g_k_att,
           ml_gate_b, g_ml_out, gla_w_a, gla_b_a, g_gla_out, w_out, g_norm_ffn, w_router, w_gate_e, w_up_e,
           w_down_e):
    bp, sp, _ = x_prompt.shape
    bs, ss, _ = x_sample.shape
    nb = bp + bs
    c_all = jnp.concatenate([c_prompt, c_sample, jnp.zeros((-nb % 8, D_MODEL), F32)], axis=0)
    mod = _ada_call(c_all, w_ada, b_ada).reshape(DEPTH, c_all.shape[0], 6, D_MODEL)
    mod = jnp.pad(mod, ((0, 0), (0, 0), (0, 2), (0, 0)))
    biases = [_attn_bias(rel_bias, d) for d in DILATIONS]
    xp = x_prompt.reshape(bp * sp, D_MODEL)
    xs = x_sample.reshape(bs * ss, D_MODEL)
    for l in range(DEPTH):
        p = _prep_layer(l, w_in, g_q_att, g_k_att, ml_gate_b, g_ml_out, gla_w_a, gla_b_a, g_gla_out, w_out,
                        g_norm_mix, g_norm_ffn, w_router, w_gate_e, w_up_e, w_down_e)
        xp = _layer(xp, mod[l, :bp], biases, p, bp, sp)
        xs = _layer(xs, mod[l, bp:nb], biases, p, bs, ss)
    return xp.reshape(bp, sp, D_MODEL), xs.reshape(bs, ss, D_MODEL)
```
